```python
import math
import jax, jax.numpy as jnp
from jax import lax
import numpy as np

D_MODEL = 1024
BATCH = 8
SEQ = 4096
DEPTH = 2

D_A = D_MODEL
K_A = 3
D_B = D_MODEL
K_B = 31
N_MEM = 256
N_HEADS = 4
HEAD_DIM = D_MODEL // N_HEADS
D_ATT = N_HEADS * HEAD_DIM
N_BRANCH = 3
D_FF = 2816
K_F = 3
EPS = 1e-6
IN_SPLITS = (3 * D_A, 3 * D_A + 2 * D_B, 3 * D_A + 2 * D_B + D_ATT)
D_IN = 3 * D_A + 2 * D_B + D_ATT + N_BRANCH * D_MODEL

kernel_name = "hybrid_shortconv_conformer_memattn_block"


def _rmsnorm(x, g):
    xf = x.astype(jnp.float32)
    r = lax.rsqrt(jnp.mean(xf * xf, axis=-1, keepdims=True) + EPS)
    return (xf * r).astype(x.dtype) * g


def _layernorm(x, g, b):
    xf = x.astype(jnp.float32)
    mu = jnp.mean(xf, axis=-1, keepdims=True)
    var = jnp.mean(jnp.square(xf - mu), axis=-1, keepdims=True)
    return ((xf - mu) * lax.rsqrt(var + EPS)).astype(x.dtype) * g + b


def _causal_dwconv(x, w):
    k, c = w.shape
    return lax.conv_general_dilated(
        x, w[:, None, :].astype(x.dtype), window_strides=(1,), padding=[(k - 1, 0)],
        dimension_numbers=("NWC", "WIO", "NWC"), feature_group_count=c)


def setup_inputs(seed: int = 0) -> dict:
    key = jax.random.key(seed)
    ks = jax.random.split(key, 24)
    L, D = DEPTH, D_MODEL

    def nrm(k, shape, fan_in):
        return jax.random.normal(k, shape, jnp.float32) * (fan_in ** -0.5)

    def gain(k, shape):
        return 1.0 + 0.02 * jax.random.normal(k, shape, jnp.float32)

    def small(k, shape):
        return 0.02 * jax.random.normal(k, shape, jnp.float32)

    return {
        "x": jax.random.normal(ks[0], (BATCH, SEQ, D), jnp.float32),
        "mem": jax.random.normal(ks[1], (BATCH, N_MEM, D), jnp.float32),
        "norm_mix_g": gain(ks[2], (L, D)),
        "norm_mem_g": gain(ks[3], (L, D)),
        "w_in": nrm(ks[4], (L, D, D_IN), D),
        "b_gate": small(ks[5], (L, N_BRANCH * D)),
        "conv_a_w": nrm(ks[6], (L, K_A, D_A), K_A),
        "w_a_out": nrm(ks[7], (L, D_A, D), D_A),
        "conv_b_w": nrm(ks[8], (L, K_B, D_B), K_B),
        "conv_b_bias": small(ks[9], (L, D_B)),
        "ln_b_g": gain(ks[10], (L, D_B)),
        "ln_b_b": small(ks[11], (L, D_B)),
        "w_b_out": nrm(ks[12], (L, D_B, D), D_B),
        "w_kv": nrm(ks[13], (L, D, 2 * D_ATT), D),
        "w_att_out": nrm(ks[14], (L, D_ATT, D), D_ATT),
        "w_o": nrm(ks[15], (L, D, D), D),
        "norm_ffn_g": gain(ks[16], (L, D)),
        "w_up": nrm(ks[17], (L, D, 2 * D_FF), D),
        "conv_ffn_w": nrm(ks[18], (L, K_F, 2 * D_FF), K_F),
        "w_down": nrm(ks[19], (L, D_FF, D), D_FF),
        "norm_final_g": gain(ks[20], (D,)),
    }


def _mixer(x, mem, norm_mix_g, norm_mem_g, w_in, b_gate, conv_a_w, w_a_out,
           conv_b_w, conv_b_bias, ln_b_g, ln_b_b, w_b_out, w_kv, w_att_out, w_o):
    bsz, seq, _ = x.shape
    h = _rmsnorm(x, norm_mix_g)
    proj = h @ w_in
    p_a, p_b, q, p_g = jnp.split(proj, IN_SPLITS, axis=-1)

    gb, gc, v = jnp.split(p_a, 3, axis=-1)
    y_a = (gb * _causal_dwconv(gc * v, conv_a_w)) @ w_a_out

    u, ug = jnp.split(p_b, 2, axis=-1)
    u = u * jax.nn.sigmoid(ug)
    u = _causal_dwconv(u, conv_b_w) + conv_b_bias
    u = jax.nn.silu(_layernorm(u, ln_b_g, ln_b_b))
    y_b = u @ w_b_out

    memn = _rmsnorm(mem, norm_mem_g)
    k, vm = jnp.split(memn @ w_kv, 2, axis=-1)
    qh = q.reshape(bsz, seq, N_HEADS, HEAD_DIM)
    kh = k.reshape(bsz, N_MEM, N_HEADS, HEAD_DIM)
    vh = vm.reshape(bsz, N_MEM, N_HEADS, HEAD_DIM)
    s = jnp.einsum("bshd,bmhd->bhsm", qh, kh).astype(jnp.float32) * (1.0 / math.sqrt(HEAD_DIM))
    pr = jax.nn.softmax(s, axis=-1).astype(x.dtype)
    o = jnp.einsum("bhsm,bmhd->bshd", pr, vh).reshape(bsz, seq, D_ATT)
    y_c = o @ w_att_out

    g = jax.nn.sigmoid((p_g + b_gate).reshape(bsz, seq, N_BRANCH, D_MODEL))
    merged = g[:, :, 0] * y_a + g[:, :, 1] * y_b + g[:, :, 2] * y_c
    return merged @ w_o


def _conv_ffn(x, norm_ffn_g, w_up, conv_ffn_w, w_down):
    h = _rmsnorm(x, norm_ffn_g)
    u = _causal_dwconv(h @ w_up, conv_ffn_w)
    gt, up = jnp.split(u, 2, axis=-1)
    return (jax.nn.silu(gt) * up) @ w_down


def reference(x, mem, norm_mix_g, norm_mem_g, w_in, b_gate, conv_a_w, w_a_out,
              conv_b_w, conv_b_bias, ln_b_g, ln_b_b, w_b_out, w_kv, w_att_out, w_o,
              norm_ffn_g, w_up, conv_ffn_w, w_down, norm_final_g):
    for l in range(DEPTH):
        x = x + _mixer(x, mem, norm_mix_g[l], norm_mem_g[l], w_in[l], b_gate[l],
                       conv_a_w[l], w_a_out[l], conv_b_w[l], conv_b_bias[l],
                       ln_b_g[l], ln_b_b[l], w_b_out[l], w_kv[l], w_att_out[l], w_o[l])
        x = x + _conv_ffn(x, norm_ffn_g[l], w_up[l], conv_ffn_w[l], w_down[l])
    return _rmsnorm(x, norm_final_g)
```

```python
import functools

import jax
import jax.numpy as jnp
from jax import lax
from jax.experimental import pallas as pl
from jax.experimental.pallas import tpu as pltpu

F32 = jnp.float32
BF16 = jnp.bfloat16

EPS = 1e-6
N_HEADS = 4
K_A = 3
K_B = 31
K_F = 3

TOKEN_TILE = 256
LANE_CHUNK = 256
SUBLANES = 8
CONV_B_HALO = 32
CONV_B_ROWS = 64
VMEM_LIMIT_BYTES = 56 * 1024 * 1024


def _dot(a, b):
    return jnp.dot(a, b, preferred_element_type=F32)


def _rmsnorm(x, g):
    r = lax.rsqrt(jnp.mean(x * x, axis=-1, keepdims=True) + EPS)
    return (x * r) * g


def _sigmoid(x):
    return 1.0 / (1.0 + jnp.exp(-x))


def _conv3(buf, w_ref, col, width, rows):
    cs = slice(col, col + width)
    out = buf[SUBLANES:SUBLANES + rows, cs] * w_ref[2:3, cs]
    out = out + buf[SUBLANES - 1:SUBLANES - 1 + rows, cs] * w_ref[1:2, cs]
    out = out + buf[SUBLANES - 2:SUBLANES - 2 + rows, cs] * w_ref[0:1, cs]
    return out


def _conv31_block(buf, w_ref, col, width, row0, rows):
    cs = slice(col, col + width)
    y = None
    for b in range(SUBLANES):
        q = None
        for o in range(CONV_B_HALO - (K_B - 1), CONV_B_HALO + 1):
            if o % SUBLANES != b:
                continue
            base = row0 + o - b
            extra = SUBLANES if b else 0
            term = buf[base:base + rows + extra, cs] * w_ref[o - 2:o - 1, cs]
            q = term if q is None else q + term
        piece = q[b:b + rows]
        y = piece if y is None else y + piece
    return y


def _kv_kernel(mem_ref, g_ref, wkv_ref, kt_ref, v_ref):
    d = mem_ref.shape[-1]
    mn = _rmsnorm(mem_ref[0], g_ref[...]).astype(BF16)
    kv = _dot(mn, wkv_ref[...])
    kt_ref[0] = kv[:, :d].T.astype(BF16)
    v_ref[0] = kv[:, d:].astype(BF16)


def _mixer_kernel(x_ref, kt_ref, v_ref, g_ref, win_ref, bg_ref, caw_ref, wa_ref, cbw_ref, cbb_ref,
                  lng_ref, lnb_ref, wb_ref, watt_ref, wo_ref, o_ref,
                  h_s, abuf, bbuf, ta_s, cb_s, ub_s, oc_s, m_s):
    t, d = h_s.shape
    cw = LANE_CHUNK
    hd = d // N_HEADS

    @pl.when(pl.program_id(1) == 0)
    def _():
        abuf[0:SUBLANES, :] = jnp.zeros((SUBLANES, d), F32)
        bbuf[0:CONV_B_HALO, :] = jnp.zeros((CONV_B_HALO, d), F32)

    h_s[...] = _rmsnorm(x_ref[0], g_ref[...]).astype(BF16)

    for c in range(0, d, cw):
        h = h_s[...]
        gc = _dot(h, win_ref[:, d + c:d + c + cw])
        vv = _dot(h, win_ref[:, 2 * d + c:2 * d + c + cw])
        abuf[SUBLANES:SUBLANES + t, c:c + cw] = gc * vv
        conv = _conv3(abuf, caw_ref, c, cw, t)
        gb = _dot(h, win_ref[:, c:c + cw])
        ta_s[:, c:c + cw] = (gb * conv).astype(BF16)
    abuf[0:SUBLANES, :] = abuf[t:t + SUBLANES, :]

    for c in range(0, d, cw):
        h = h_s[...]
        u = _dot(h, win_ref[:, 3 * d + c:3 * d + c + cw])
        ug = _dot(h, win_ref[:, 4 * d + c:4 * d + c + cw])
        bbuf[CONV_B_HALO:CONV_B_HALO + t, c:c + cw] = u * _sigmoid(ug)
        for r0 in range(0, t, CONV_B_ROWS):
            y = _conv31_block(bbuf, cbw_ref, c, cw, r0, CONV_B_ROWS)
            cb_s[r0:r0 + CONV_B_ROWS, c:c + cw] = y + cbb_ref[:, c:c + cw]
    bbuf[0:CONV_B_HALO, :] = bbuf[t:t + CONV_B_HALO, :]
    cb = cb_s[...]
    mu = jnp.mean(cb, axis=-1, keepdims=True)
    cen = cb - mu
    var = jnp.mean(cen * cen, axis=-1, keepdims=True)
    ln = (cen * lax.rsqrt(var + EPS)) * lng_ref[...] + lnb_ref[...]
    ub_s[...] = (ln * _sigmoid(ln)).astype(BF16)

    for i in range(N_HEADS):
        h = h_s[...]
        q = _dot(h, win_ref[:, 5 * d + i * hd:5 * d + (i + 1) * hd])
        s = _dot(q.astype(BF16), kt_ref[0, i * hd:(i + 1) * hd, :]) * (1.0 / (hd ** 0.5))
        e = jnp.exp(s - jnp.max(s, axis=-1, keepdims=True))
        p = e / jnp.sum(e, axis=-1, keepdims=True)
        oc_s[:, i * hd:(i + 1) * hd] = _dot(p.astype(BF16), v_ref[0, :, i * hd:(i + 1) * hd]).astype(BF16)

    for c in range(0, d, cw):
        h = h_s[...]
        y_a = _dot(ta_s[...], wa_ref[:, c:c + cw])
        g_a = _sigmoid(_dot(h, win_ref[:, 6 * d + c:6 * d + c + cw]) + bg_ref[:, c:c + cw])
        m = g_a * y_a
        y_b = _dot(ub_s[...], wb_ref[:, c:c + cw])
        g_b = _sigmoid(_dot(h, win_ref[:, 7 * d + c:7 * d + c + cw]) + bg_ref[:, d + c:d + c + cw])
        m = m + g_b * y_b
        y_c = _dot(oc_s[...], watt_ref[:, c:c + cw])
        g_c = _sigmoid(_dot(h, win_ref[:, 8 * d + c:8 * d + c + cw]) + bg_ref[:, 2 * d + c:2 * d + c + cw])
        m = m + g_c * y_c
        m_s[:, c:c + cw] = m.astype(BF16)

    for c in range(0, d, cw):
        o_ref[0, :, c:c + cw] = x_ref[0, :, c:c + cw] + _dot(m_s[...], wo_ref[:, c:c + cw])


def _ffn_kernel(x_ref, g_ref, wup_ref, cfw_ref, wdn_ref, gf_ref, o_ref, h_s, fbuf, a_s, y_s, *, final_norm):
    t, d = h_s.shape
    dff = a_s.shape[-1]
    cw = LANE_CHUNK

    @pl.when(pl.program_id(1) == 0)
    def _():
        fbuf[0:SUBLANES, :] = jnp.zeros((SUBLANES, 2 * dff), F32)

    h_s[...] = _rmsnorm(x_ref[0], g_ref[...]).astype(BF16)
    for c in range(0, dff, cw):
        h = h_s[...]
        fbuf[SUBLANES:SUBLANES + t, c:c + cw] = _dot(h, wup_ref[:, c:c + cw])
        fbuf[SUBLANES:SUBLANES + t, dff + c:dff + c + cw] = _dot(h, wup_ref[:, dff + c:dff + c + cw])
        gt = _conv3(fbuf, cfw_ref, c, cw, t)
        up = _conv3(fbuf, cfw_ref, dff + c, cw, t)
        a_s[:, c:c + cw] = ((gt * _sigmoid(gt)) * up).astype(BF16)
    fbuf[0:SUBLANES, :] = fbuf[t:t + SUBLANES, :]

    for c in range(0, d, cw):
        y = x_ref[0, :, c:c + cw] + _dot(a_s[...], wdn_ref[:, c:c + cw])
        if final_norm:
            y_s[:, c:c + cw] = y
        else:
            o_ref[0, :, c:c + cw] = y
    if final_norm:
        o_ref[0] = _rmsnorm(y_s[...], gf_ref[...])


def _resident(shape):
    return pl.BlockSpec(shape, lambda *_: (0,) * len(shape), pipeline_mode=pl.Buffered(1))


def _compiler_params(semantics):
    return pltpu.CompilerParams(dimension_semantics=semantics, vmem_limit_bytes=VMEM_LIMIT_BYTES)


def _project_memory(mem, g, w_kv):
    b, m, d = mem.shape
    return pl.pallas_call(
        _kv_kernel,
        grid=(b,),
        in_specs=[pl.BlockSpec((1, m, d), lambda i: (i, 0, 0)), _resident((1, d)), _resident((d, 2 * d))],
        out_specs=[pl.BlockSpec((1, d, m), lambda i: (i, 0, 0)), pl.BlockSpec((1, m, d), lambda i: (i, 0, 0))],
        out_shape=[jax.ShapeDtypeStruct((b, d, m), BF16), jax.ShapeDtypeStruct((b, m, d), BF16)],
        compiler_params=_compiler_params(("arbitrary",)),
        name="project_memory",
    )(mem, g, w_kv)


def _mixer(x, kt, v, g, w_in, b_gate, conv_a_w, w_a_out, conv_b_w, conv_b_bias, ln_g, ln_b, w_b_out, w_att_out, w_o):
    b, s, d = x.shape
    m = v.shape[1]
    t = TOKEN_TILE
    tile = pl.BlockSpec((1, t, d), lambda i, j: (i, j, 0))
    return pl.pallas_call(
        _mixer_kernel,
        grid=(b, s // t),
        in_specs=[tile,
                  pl.BlockSpec((1, d, m), lambda i, j: (i, 0, 0)),
                  pl.BlockSpec((1, m, d), lambda i, j: (i, 0, 0)),
                  _resident((1, d)), _resident(w_in.shape), _resident(b_gate.shape),
                  _resident(conv_a_w.shape), _resident((d, d)), _resident(conv_b_w.shape), _resident((1, d)),
                  _resident((1, d)), _resident((1, d)), _resident((d, d)), _resident((d, d)), _resident((d, d))],
        out_specs=tile,
        out_shape=jax.ShapeDtypeStruct(x.shape, F32),
        scratch_shapes=[pltpu.VMEM((t, d), BF16),
                        pltpu.VMEM((t + SUBLANES, d), F32),
                        pltpu.VMEM((t + CONV_B_HALO, d), F32),
                        pltpu.VMEM((t, d), BF16),
                        pltpu.VMEM((t, d), F32),
                        pltpu.VMEM((t, d), BF16),
                        pltpu.VMEM((t, d), BF16),
                        pltpu.VMEM((t, d), BF16)],
        compiler_params=_compiler_params(("arbitrary", "arbitrary")),
        name="mixer",
    )(x, kt, v, g, w_in, b_gate, conv_a_w, w_a_out, conv_b_w, conv_b_bias, ln_g, ln_b, w_b_out, w_att_out, w_o)


def _conv_ffn(x, g, w_up, conv_w, w_down, g_final, final_norm):
    b, s, d = x.shape
    dff = w_down.shape[0]
    t = TOKEN_TILE
    tile = pl.BlockSpec((1, t, d), lambda i, j: (i, j, 0))
    return pl.pallas_call(
        functools.partial(_ffn_kernel, final_norm=final_norm),
        grid=(b, s // t),
        in_specs=[tile, _resident((1, d)), _resident(w_up.shape), _resident(conv_w.shape),
                  _resident(w_down.shape), _resident((1, d))],
        out_specs=tile,
        out_shape=jax.ShapeDtypeStruct(x.shape, F32),
        scratch_shapes=[pltpu.VMEM((t, d), BF16),
                        pltpu.VMEM((t + SUBLANES, 2 * dff), F32),
                        pltpu.VMEM((t, dff), BF16),
                        pltpu.VMEM((t, d), F32)],
        compiler_params=_compiler_params(("arbitrary", "arbitrary")),
        name="conv_ffn_final" if final_norm else "conv_ffn",
    )(x, g, w_up, conv_w, w_down, g_final)


def kernel(x, mem, norm_mix_g, norm_mem_g, w_in, b_gate, conv_a_w, w_a_out, conv_b_w, conv_b_bias, ln_b_g, ln_b_b, w_b_out, w_kv, w_att_out, w_o, norm_ffn_g, w_up, conv_ffn_w, w_down, norm_final_g):
    depth = w_in.shape[0]
    row = lambda p: p.reshape(1, -1)
    for l in range(depth):
        kt, v = _project_memory(mem, row(norm_mem_g[l]), w_kv[l].astype(BF16))
        x = _mixer(x, kt, v, row(norm_mix_g[l]), w_in[l].astype(BF16), row(b_gate[l]), conv_a_w[l],
                   w_a_out[l].astype(BF16), conv_b_w[l], row(conv_b_bias[l]), row(ln_b_g[l]), row(ln_b_b[l]),
                   w_b_out[l].astype(BF16), w_att_out[l].astype(BF16), w_o[l].astype(BF16))
        x = _conv_ffn(x, row(norm_ffn_g[l]), w_up[l].astype(BF16), conv_ffn_w[l], w_down[l].astype(BF16),
                      row(norm_final_g), final_norm=(l == depth - 1))
    return x
```
